```python
import jax, jax.numpy as jnp
from jax import lax
import numpy as np

D_MODEL = 2048
BATCH = 8
SEQ = 4096
DEPTH = 4

GRID_W = 64
CTX_LEN = 256
N_MIXERS = 4
EPS = 1e-6
NEG_INF = -1e30
FFN_HIDDEN = 5632
NA_HEADS = 16
NA_HEAD_DIM = D_MODEL // NA_HEADS
NA_WIN_ROWS = 8
NA_WIN_COLS = 16
CV_KERNEL = 31
GA_HEADS = 16
GA_KV_HEADS = 4
GA_HEAD_DIM = D_MODEL // GA_HEADS
GA_Q_BLOCK = 128
ROPE_THETA = 10000.0
LRU_WIDTH = (D_MODEL * 5) // 4
LRU_BLOCKS = 10
LRU_BLOCK = LRU_WIDTH // LRU_BLOCKS
LRU_CONV = 4
LRU_C = 8.0
LRU_A_MIN = 0.9
LRU_A_MAX = 0.999

kernel_name = 'hybrid_natten_conformer_gqa_rglru_dit'


def _n_layers_of(kind):
    return len(range(kind, DEPTH, N_MIXERS))


def _rmsnorm(x, g):
    xf = x.astype(jnp.float32)
    y = xf * lax.rsqrt(jnp.mean(xf * xf, axis=-1, keepdims=True) + EPS)
    return (y * g.astype(jnp.float32)).astype(x.dtype)


def _adanorm(x, g, mod, s):
    return _rmsnorm(x, g) * (1 + mod[:, :, s, 1]) + mod[:, :, s, 0]


def _swiglu(x, w13, w2):
    g, u = jnp.split(x @ w13, 2, axis=-1)
    return (jax.nn.silu(g) * u) @ w2


def _dwconv(x, w, pad):
    return lax.conv_general_dilated(x, w[:, None, :], window_strides=(1,), padding=[pad],
                                    dimension_numbers=('NWC', 'WIO', 'NWC'),
                                    feature_group_count=x.shape[-1])


def _ctx_attn(qc, kc, vc):
    B, L, H, dh = qc.shape
    Hk = kc.shape[2]
    qg = qc.reshape(B, L, Hk, H // Hk, dh)
    s = jnp.einsum('blkgd,bmkd->bkglm', qg, kc).astype(jnp.float32) * (dh ** -0.5)
    p = jax.nn.softmax(s, axis=-1).astype(vc.dtype)
    o = jnp.einsum('bkglm,bmkd->blkgd', p, vc)
    return o.reshape(B, L, H * dh)


def _axial_rope(x):
    T, dh = x.shape[1], x.shape[-1]
    t = jnp.arange(T)
    half = dh // 2
    quarter = half // 2
    inv = ROPE_THETA ** (-jnp.arange(quarter, dtype=jnp.float32) / quarter)
    outs = []
    for a, pos in enumerate((t // GRID_W, t % GRID_W)):
        ang = pos.astype(jnp.float32)[:, None] * inv[None, :]
        cos = jnp.cos(ang)[None, :, None, :]
        sin = jnp.sin(ang)[None, :, None, :]
        xa = x[..., a * half:(a + 1) * half].astype(jnp.float32)
        x1, x2 = xa[..., :quarter], xa[..., quarter:]
        outs += [x1 * cos - x2 * sin, x2 * cos + x1 * sin]
    return jnp.concatenate(outs, axis=-1).astype(x.dtype)


def _natten_mixer(xn, hn, w_qkv, w_o, rpb, ctx_out):
    B, T, D = xn.shape
    L = hn.shape[1]
    H, dh = NA_HEADS, NA_HEAD_DIM
    rows = T // GRID_W
    kr = min(NA_WIN_ROWS, rows)
    scale = dh ** -0.5
    qkv = (xn @ w_qkv).reshape(B, rows, GRID_W, 3, H, dh)
    q, k, v = qkv[:, :, :, 0], qkv[:, :, :, 1], qkv[:, :, :, 2]
    kvc = (hn @ w_qkv[:, D:]).reshape(B, L, 2, H, dh)
    kc, vc = kvc[:, :, 0], kvc[:, :, 1]
    cols = jnp.arange(GRID_W)
    cstart = jnp.clip(cols - NA_WIN_COLS // 2, 0, GRID_W - NA_WIN_COLS)
    col_in = (cols[None, :] >= cstart[:, None]) & (cols[None, :] < cstart[:, None] + NA_WIN_COLS)
    col_idx = jnp.clip(cols[None, :] - cols[:, None] + NA_WIN_COLS - 1, 0, 2 * NA_WIN_COLS - 2)
    rpb_cols = rpb[:, :, col_idx]

    def row_step(r):
        rs = jnp.clip(r - kr // 2, 0, rows - kr)
        kb = lax.dynamic_slice_in_dim(k, rs, kr, axis=1)
        vb = lax.dynamic_slice_in_dim(v, rs, kr, axis=1)
        qr = lax.dynamic_index_in_dim(q, r, axis=1, keepdims=False)
        s = jnp.einsum('bqhd,bjkhd->bhqjk', qr, kb).astype(jnp.float32) * scale
        roff = rs + jnp.arange(kr) - r + NA_WIN_ROWS - 1
        bias = jnp.transpose(rpb_cols[:, roff], (0, 2, 1, 3)).astype(jnp.float32)
        s = jnp.where(col_in[:, None, :], s + bias[None], NEG_INF)
        s = s.reshape(B, H, GRID_W, kr * GRID_W)
        sc = jnp.einsum('bqhd,blhd->bhql', qr, kc).astype(jnp.float32) * scale
        p = jax.nn.softmax(jnp.concatenate([s, sc], axis=-1), axis=-1).astype(v.dtype)
        o = jnp.einsum('bhqn,bnhd->bqhd', p[..., :kr * GRID_W], vb.reshape(B, kr * GRID_W, H, dh))
        return o + jnp.einsum('bhql,blhd->bqhd', p[..., kr * GRID_W:], vc)

    o = lax.map(row_step, jnp.arange(rows))
    y = jnp.moveaxis(o, 0, 1).reshape(B, T, D) @ w_o
    yc = None
    if ctx_out:
        qc = (hn @ w_qkv[:, :D]).reshape(B, L, H, dh)
        yc = _ctx_attn(qc, kc, vc) @ w_o
    return y, yc


def _conv_module(u, w1, b1, wdw, bdw, ln_g, ln_b, w2, b2):
    a, g = jnp.split(u @ w1 + b1, 2, axis=-1)
    z = a * jax.nn.sigmoid(g)
    z = _dwconv(z, wdw, (CV_KERNEL // 2, CV_KERNEL // 2)) + bdw
    zf = z.astype(jnp.float32)
    mu = jnp.mean(zf, axis=-1, keepdims=True)
    var = jnp.mean(jnp.square(zf - mu), axis=-1, keepdims=True)
    z = ((zf - mu) * lax.rsqrt(var + EPS) * ln_g + ln_b).astype(u.dtype)
    return jax.nn.silu(z) @ w2 + b2


def _gqa_mixer(xn, hn, wq, wkv, wo, gq, gk, ctx_out):
    B, T, D = xn.shape
    L = hn.shape[1]
    H, Hk, dh = GA_HEADS, GA_KV_HEADS, GA_HEAD_DIM
    G = H // Hk
    scale = dh ** -0.5

    def proj_kv(z, n):
        kv = (z @ wkv).reshape(B, n, 2, Hk, dh)
        return _rmsnorm(kv[:, :, 0], gk), kv[:, :, 1]

    q = _axial_rope(_rmsnorm((xn @ wq).reshape(B, T, H, dh), gq))
    k, v = proj_kv(xn, T)
    k = _axial_rope(k)
    kc, vc = proj_kv(hn, L)
    k_all = jnp.concatenate([kc, k], axis=1)
    v_all = jnp.concatenate([vc, v], axis=1)
    nb = T // GA_Q_BLOCK
    qb = jnp.moveaxis(q.reshape(B, nb, GA_Q_BLOCK, Hk, G, dh), 1, 0)

    def block(qi):
        s = jnp.einsum('bqkgd,bskd->bkgqs', qi, k_all).astype(jnp.float32) * scale
        p = jax.nn.softmax(s, axis=-1).astype(v_all.dtype)
        return jnp.einsum('bkgqs,bskd->bqkgd', p, v_all)

    o = lax.map(block, qb)
    y = jnp.moveaxis(o, 0, 1).reshape(B, T, D) @ wo
    yc = None
    if ctx_out:
        qc = _rmsnorm((hn @ wq).reshape(B, L, H, dh), gq)
        yc = _ctx_attn(qc, kc, vc) @ wo
    return y, yc


def _rglru_bidir(u, conv_w, conv_b, wa, ba, wi, bi, lam, h0):
    B, T, R = u.shape
    ud = jnp.stack([u, u[:, ::-1]], axis=0)
    ud = jax.vmap(lambda z, w: _dwconv(z, w, (LRU_CONV - 1, 0)))(ud, conv_w) + conv_b[:, None, None, :]
    zb = ud.reshape(2, B, T, LRU_BLOCKS, LRU_BLOCK)
    r = jax.nn.sigmoid(jnp.einsum('ebtnk,enkj->ebtnj', zb, wa).reshape(2, B, T, R) + ba[:, None, None, :])
    ig = jax.nn.sigmoid(jnp.einsum('ebtnk,enkj->ebtnj', zb, wi).reshape(2, B, T, R) + bi[:, None, None, :])
    log_a = -LRU_C * r.astype(jnp.float32) * jax.nn.softplus(-lam.astype(jnp.float32))[:, None, None, :]
    a = jnp.exp(log_a)
    b = jnp.sqrt(-jnp.expm1(2.0 * log_a)) * (ig * ud).astype(jnp.float32)

    def step(h, ab):
        h = ab[0] * h + ab[1]
        return h, h

    h_last, hs = lax.scan(step, h0, (jnp.moveaxis(a, 2, 0), jnp.moveaxis(b, 2, 0)))
    y = hs[:, 0] + hs[::-1, 1]
    return jnp.moveaxis(y, 0, 1), h_last


def _lru_mixer(xn, hn, w_in, conv_w, conv_b, wa, ba, wi, bi, lam, w_out, ctx_out):
    B = xn.shape[0]
    R = LRU_WIDTH
    uc = hn @ w_in[:, :R]
    h0 = jnp.zeros((2, B, R), jnp.float32)
    yc_rec, h_ctx = _rglru_bidir(uc, conv_w, conv_b, wa, ba, wi, bi, lam, h0)
    ux, ug = jnp.split(xn @ w_in, 2, axis=-1)
    y_rec, _ = _rglru_bidir(ux, conv_w, conv_b, wa, ba, wi, bi, lam, h_ctx)
    y = (jax.nn.gelu(ug) * y_rec.astype(xn.dtype)) @ w_out
    yc = None
    if ctx_out:
        yc = (jax.nn.gelu(hn @ w_in[:, R:]) * yc_rec.astype(hn.dtype)) @ w_out
    return y, yc


def setup_inputs(seed: int = 0) -> dict:
    key = jax.random.key(seed)
    keys = iter(jax.random.split(key, 40))
    f32 = jnp.float32
    D, F, R = D_MODEL, FFN_HIDDEN, LRU_WIDTH
    nA, nB, nC, nD = (_n_layers_of(kd) for kd in range(N_MIXERS))

    def normal(shape, s=1.0):
        return s * jax.random.normal(next(keys), shape, f32)

    def dense(shape, fan_in, s=1.0):
        return normal(shape, s * fan_in ** -0.5)

    def gain(shape):
        return 1.0 + normal(shape, 0.02)

    def bias(shape):
        return normal(shape, 0.02)

    def lru_lambda(shape):
        a = jax.random.uniform(next(keys), shape, f32, LRU_A_MIN, LRU_A_MAX) ** (1.0 / LRU_C)
        return jnp.log(a) - jnp.log1p(-a)

    return {
        'x': normal((BATCH, SEQ, D)),
        'c': normal((BATCH, D)),
        'ctx': normal((BATCH, CTX_LEN, D)),
        'c_ctx': normal((D,)),
        'ada_w': dense((DEPTH, D, 9 * D), D, 0.3),
        'ada_b': bias((DEPTH, 9 * D)),
        'norm_g': gain((DEPTH, 3, D)),
        'ffn_w13': dense((DEPTH, 2, D, 2 * F), D),
        'ffn_w2': dense((DEPTH, 2, F, D), F),
        'final_g': gain((D,)),
        'na_wqkv': dense((nA, D, 3 * D), D),
        'na_wo': dense((nA, D, D), D),
        'na_rpb': normal((nA, NA_HEADS, 2 * NA_WIN_ROWS - 1, 2 * NA_WIN_COLS - 1), 0.1),
        'cv_w1': dense((nB, D, 2 * D), D),
        'cv_b1': bias((nB, 2 * D)),
        'cv_wdw': dense((nB, CV_KERNEL, D), CV_KERNEL),
        'cv_bdw': bias((nB, D)),
        'cv_ln_g': gain((nB, D)),
        'cv_ln_b': bias((nB, D)),
        'cv_w2': dense((nB, D, D), D),
        'cv_b2': bias((nB, D)),
        'ga_wq': dense((nC, D, GA_HEADS * GA_HEAD_DIM), D),
        'ga_wkv': dense((nC, D, 2 * GA_KV_HEADS * GA_HEAD_DIM), D),
        'ga_wo': dense((nC, GA_HEADS * GA_HEAD_DIM, D), D),
        'ga_gq': gain((nC, GA_HEAD_DIM)),
        'ga_gk': gain((nC, GA_HEAD_DIM)),
        'lru_win': dense((nD, D, 2 * R), D),
        'lru_convw': dense((nD, 2, LRU_CONV, R), LRU_CONV),
        'lru_convb': bias((nD, 2, R)),
        'lru_wa': dense((nD, 2, LRU_BLOCKS, LRU_BLOCK, LRU_BLOCK), LRU_BLOCK),
        'lru_ba': bias((nD, 2, R)),
        'lru_wi': dense((nD, 2, LRU_BLOCKS, LRU_BLOCK, LRU_BLOCK), LRU_BLOCK),
        'lru_bi': bias((nD, 2, R)),
        'lru_lam': lru_lambda((nD, 2, R)),
        'lru_wout': dense((nD, R, D), R),
    }


def reference(x, c, ctx, c_ctx, ada_w, ada_b, norm_g, ffn_w13, ffn_w2, final_g,
              na_wqkv, na_wo, na_rpb,
              cv_w1, cv_b1, cv_wdw, cv_bdw, cv_ln_g, cv_ln_b, cv_w2, cv_b2,
              ga_wq, ga_wkv, ga_wo, ga_gq, ga_gk,
              lru_win, lru_convw, lru_convb, lru_wa, lru_ba, lru_wi, lru_bi, lru_lam, lru_wout):
    B, T, D = x.shape
    h = ctx
    s_lat = jax.nn.silu(c)
    s_ctx = jax.nn.silu(c_ctx)
    for i in range(DEPTH):
        kind, j = i % N_MIXERS, i // N_MIXERS
        last = i == DEPTH - 1
        ctx_in = not (last and kind == 1)
        ctx_out = not last
        mx = (s_lat @ ada_w[i] + ada_b[i]).reshape(B, 1, 3, 3, D)
        mh = (s_ctx @ ada_w[i] + ada_b[i]).reshape(1, 1, 3, 3, D)
        x = x + 0.5 * mx[:, :, 0, 2] * _swiglu(_adanorm(x, norm_g[i, 0], mx, 0), ffn_w13[i, 0], ffn_w2[i, 0])
        if ctx_in:
            h = h + 0.5 * mh[:, :, 0, 2] * _swiglu(_adanorm(h, norm_g[i, 0], mh, 0), ffn_w13[i, 0], ffn_w2[i, 0])
        xn = _adanorm(x, norm_g[i, 1], mx, 1)
        hn = _adanorm(h, norm_g[i, 1], mh, 1) if ctx_in else None
        if kind == 0:
            yx, yh = _natten_mixer(xn, hn, na_wqkv[j], na_wo[j], na_rpb[j], ctx_out)
        elif kind == 1:
            cvp = (cv_w1[j], cv_b1[j], cv_wdw[j], cv_bdw[j], cv_ln_g[j], cv_ln_b[j], cv_w2[j], cv_b2[j])
            yx = _conv_module(xn, *cvp)
            yh = _conv_module(hn, *cvp) if ctx_out else None
        elif kind == 2:
            yx, yh = _gqa_mixer(xn, hn, ga_wq[j], ga_wkv[j], ga_wo[j], ga_gq[j], ga_gk[j], ctx_out)
        else:
            yx, yh = _lru_mixer(xn, hn, lru_win[j], lru_convw[j], lru_convb[j], lru_wa[j], lru_ba[j],
                                lru_wi[j], lru_bi[j], lru_lam[j], lru_wout[j], ctx_out)
        x = x + mx[:, :, 1, 2] * yx
        x = x + 0.5 * mx[:, :, 2, 2] * _swiglu(_adanorm(x, norm_g[i, 2], mx, 2), ffn_w13[i, 1], ffn_w2[i, 1])
        if ctx_out:
            h = h + mh[:, :, 1, 2] * yh
            h = h + 0.5 * mh[:, :, 2, 2] * _swiglu(_adanorm(h, norm_g[i, 2], mh, 2), ffn_w13[i, 1], ffn_w2[i, 1])
    return _rmsnorm(x, final_g)
```

```python
import functools

import numpy as np
import jax
import jax.numpy as jnp
from jax import lax
from jax.experimental import pallas as pl
from jax.experimental.pallas import tpu as pltpu

F32 = jnp.float32
BF16 = jnp.bfloat16

GRID_W = 64
EPS = 1e-6
NEG_INF = -1e30
HEAD_DIM = 128
NA_HEADS = 16
NA_WIN_ROWS = 8
NA_WIN_COLS = 16
CV_KERNEL = 31
GA_HEADS = 16
GA_KV_HEADS = 4
ROPE_THETA = 10000.0
LRU_BLOCKS = 10
LRU_CONV = 4
LRU_C = 8.0
N_MIXERS = 4

V7X_VMEM_LIMIT_BYTES = 56 * 1024 * 1024
SUBLANES = 8
CONV_HALO = 16
LRU_HALO = 8


def _cparams(*sem):
    return pltpu.CompilerParams(dimension_semantics=sem, vmem_limit_bytes=V7X_VMEM_LIMIT_BYTES)


def _dot(a, b):
    return jnp.dot(a, b, preferred_element_type=F32)


def _dot_nt(a, b):
    return lax.dot_general(a, b, (((1,), (1,)), ((), ())), preferred_element_type=F32)


def _sigmoid(x):
    return jax.nn.sigmoid(x)


def _rms(x, g):
    return x * lax.rsqrt(jnp.mean(x * x, axis=-1, keepdims=True) + EPS) * g


def _adanorm(x, g, mod_ref, sub):
    shift = mod_ref[3 * sub:3 * sub + 1, :]
    scale = mod_ref[3 * sub + 1:3 * sub + 2, :]
    return _rms(x, g) * (1.0 + scale) + shift


def _gate_row(mod_ref, sub):
    return mod_ref[3 * sub + 2:3 * sub + 3, :]


def _ada_kernel(c_ref, w_ref, b_ref, o_ref):
    c = c_ref[...]
    s = (c * _sigmoid(c)).astype(BF16)
    o_ref[...] = _dot(s, w_ref[...].astype(BF16)) + b_ref[...]


def _ada_mod(cvec, ada_w, ada_b):
    depth, d, _ = ada_w.shape
    s = cvec.shape[0]
    out = pl.pallas_call(
        _ada_kernel,
        grid=(depth, 9),
        in_specs=[
            pl.BlockSpec((s, d), lambda i, j: (0, 0)),
            pl.BlockSpec((None, d, d), lambda i, j: (i, 0, j)),
            pl.BlockSpec((None, 1, d), lambda i, j: (i, 0, j)),
        ],
        out_specs=pl.BlockSpec((None, None, s, d), lambda i, j: (i, j, 0, 0)),
        out_shape=jax.ShapeDtypeStruct((depth, 9, s, d), F32),
        compiler_params=_cparams("parallel", "arbitrary"),
        name="ada_mod",
    )(cvec, ada_w, ada_b.reshape(depth, 1, 9 * d))
    return jnp.transpose(out, (0, 2, 1, 3))


def _swiglu_kernel(*refs, sub, final, nf):
    if final:
        x_ref, mod_ref, g_ref, w1_ref, w3_ref, w2_ref, fg_ref, o_ref, xn_ref, acc_ref = refs
    else:
        x_ref, mod_ref, g_ref, w1_ref, w3_ref, w2_ref, o_ref, xn_ref, acc_ref = refs
    j = pl.program_id(1)

    @pl.when(j == 0)
    def _():
        xn_ref[...] = _adanorm(x_ref[...], g_ref[...], mod_ref, sub).astype(BF16)
        acc_ref[...] = jnp.zeros_like(acc_ref)

    xn = xn_ref[...]
    gt = _dot(xn, w1_ref[...])
    up = _dot(xn, w3_ref[...])
    hid = (gt * _sigmoid(gt)) * up
    acc_ref[...] += _dot(hid.astype(BF16), w2_ref[...])

    @pl.when(j == nf - 1)
    def _():
        y = x_ref[...] + (0.5 * _gate_row(mod_ref, sub)) * acc_ref[...]
        if final:
            y = _rms(y, fg_ref[...])
        o_ref[...] = y


def _swiglu(x, mod, g, w13, w2, *, sub, tps, tm, tf, final_g=None):
    n, d = x.shape
    f = w2.shape[0]
    nf = f // tf
    final = final_g is not None
    in_specs = [
        pl.BlockSpec((tm, d), lambda i, j: (i, 0)),
        pl.BlockSpec((None, 9, d), lambda i, j: (i // tps, 0, 0)),
        pl.BlockSpec((1, d), lambda i, j: (0, 0)),
        pl.BlockSpec((d, tf), lambda i, j: (0, j)),
        pl.BlockSpec((d, tf), lambda i, j: (0, j + nf)),
        pl.BlockSpec((tf, d), lambda i, j: (j, 0)),
    ]
    args = [x, mod, g.reshape(1, d), w13, w13, w2]
    if final:
        in_specs.append(pl.BlockSpec((1, d), lambda i, j: (0, 0)))
        args.append(final_g.reshape(1, d))
    return pl.pallas_call(
        functools.partial(_swiglu_kernel, sub=sub, final=final, nf=nf),
        grid=(n // tm, nf),
        in_specs=in_specs,
        out_specs=pl.BlockSpec((tm, d), lambda i, j: (i, 0)),
        out_shape=jax.ShapeDtypeStruct((n, d), F32),
        scratch_shapes=[pltpu.VMEM((tm, d), BF16), pltpu.VMEM((tm, d), F32)],
        compiler_params=_cparams("parallel", "arbitrary"),
        name="swiglu",
    )(*args)


def _proj_kernel(*refs, sub, glu, has_bias):
    x_ref, mod_ref, g_ref = refs[:3]
    rest = list(refs[3:])
    wa_ref = rest.pop(0)
    wg_ref = rest.pop(0) if glu else None
    ba_ref = rest.pop(0) if has_bias else None
    bg_ref = rest.pop(0) if (has_bias and glu) else None
    o_ref, xn_ref = rest

    @pl.when(pl.program_id(1) == 0)
    def _():
        xn_ref[...] = _adanorm(x_ref[...], g_ref[...], mod_ref, sub).astype(BF16)

    xn = xn_ref[...]
    y = _dot(xn, wa_ref[...])
    if has_bias:
        y = y + ba_ref[...]
    if glu:
        gt = _dot(xn, wg_ref[...])
        if has_bias:
            gt = gt + bg_ref[...]
        y = y * _sigmoid(gt)
    o_ref[...] = y.astype(o_ref.dtype)


def _proj(x, mod, g, w, bias=None, *, n_out, glu=False, out_dtype, tps, tm, tn):
    n, d = x.shape
    nj = n_out // tn
    has_bias = bias is not None
    in_specs = [
        pl.BlockSpec((tm, d), lambda i, j: (i, 0)),
        pl.BlockSpec((None, 9, d), lambda i, j: (i // tps, 0, 0)),
        pl.BlockSpec((1, d), lambda i, j: (0, 0)),
        pl.BlockSpec((d, tn), lambda i, j: (0, j)),
    ]
    args = [x, mod, g.reshape(1, d), w]
    if glu:
        in_specs.append(pl.BlockSpec((d, tn), lambda i, j: (0, j + nj)))
        args.append(w)
    if has_bias:
        b2 = bias.reshape(1, -1)
        in_specs.append(pl.BlockSpec((1, tn), lambda i, j: (0, j)))
        args.append(b2)
        if glu:
            in_specs.append(pl.BlockSpec((1, tn), lambda i, j: (0, j + nj)))
            args.append(b2)
    return pl.pallas_call(
        functools.partial(_proj_kernel, sub=1, glu=glu, has_bias=has_bias),
        grid=(n // tm, nj),
        in_specs=in_specs,
        out_specs=pl.BlockSpec((tm, tn), lambda i, j: (i, j)),
        out_shape=jax.ShapeDtypeStruct((n, n_out), out_dtype),
        scratch_shapes=[pltpu.VMEM((tm, d), BF16)],
        compiler_params=_cparams("parallel", "arbitrary"),
        name="adanorm_proj",
    )(*args)


def _outres_kernel(*refs, mode, has_bias):
    refs = list(refs)
    if mode == "gelu_mul":
        ug_ref, yr_ref = refs.pop(0), refs.pop(0)
        a = (jax.nn.gelu(ug_ref[...], approximate=True) * yr_ref[...]).astype(BF16)
    else:
        a = refs.pop(0)[...]
    w_ref = refs.pop(0)
    b_ref = refs.pop(0) if has_bias else None
    x_ref, mod_ref, o_ref = refs
    y = _dot(a, w_ref[...])
    if has_bias:
        y = y + b_ref[...]
    o_ref[...] = x_ref[...] + _gate_row(mod_ref, 1) * y


def _outres(a_list, a_specs, w, bias, x, mod, *, mode, tps, tm):
    n, d = x.shape
    k = w.shape[0]
    has_bias = bias is not None
    in_specs = list(a_specs) + [pl.BlockSpec((k, d), lambda i: (0, 0))]
    args = list(a_list) + [w]
    if has_bias:
        in_specs.append(pl.BlockSpec((1, d), lambda i: (0, 0)))
        args.append(bias.reshape(1, d))
    in_specs += [
        pl.BlockSpec((tm, d), lambda i: (i, 0)),
        pl.BlockSpec((None, 9, d), lambda i: (i // tps, 0, 0)),
    ]
    args += [x, mod]
    return pl.pallas_call(
        functools.partial(_outres_kernel, mode=mode, has_bias=has_bias),
        grid=(n // tm,),
        in_specs=in_specs,
        out_specs=pl.BlockSpec((tm, d), lambda i: (i, 0)),
        out_shape=jax.ShapeDtypeStruct((n, d), F32),
        compiler_params=_cparams("parallel"),
        name="mixer_out_residual",
    )(*args)


def _attn_kernel(*refs, scale, has_ctx):
    if has_ctx:
        q_ref, k_ref, v_ref, kc_ref, vc_ref, o_ref = refs
    else:
        q_ref, k_ref, v_ref, o_ref = refs
    q = q_ref[...]
    s = _dot_nt(q, k_ref[...]) * scale
    m = jnp.max(s, axis=-1, keepdims=True)
    if has_ctx:
        sc = _dot_nt(q, kc_ref[...]) * scale
        m = jnp.maximum(m, jnp.max(sc, axis=-1, keepdims=True))
    p = jnp.exp(s - m)
    l = jnp.sum(p, axis=-1, keepdims=True)
    o = _dot(p.astype(BF16), v_ref[...])
    if has_ctx:
        pc = jnp.exp(sc - m)
        l = l + jnp.sum(pc, axis=-1, keepdims=True)
        o = o + _dot(pc.astype(BF16), vc_ref[...])
    o_ref[...] = (o / l).astype(o_ref.dtype)


def _attention(qkv, seq, nb, *, heads, group, k_col, v_col, tq, ctx=None, ctx_len=None):
    dh = HEAD_DIM
    nq = seq // tq
    has_ctx = ctx is not None
    in_specs = [
        pl.BlockSpec((tq, dh), lambda b, h, i: (b * nq + i, h)),
        pl.BlockSpec((seq, dh), lambda b, h, i: (b, k_col + h // group)),
        pl.BlockSpec((seq, dh), lambda b, h, i: (b, v_col + h // group)),
    ]
    args = [qkv, qkv, qkv]
    if has_ctx:
        in_specs += [
            pl.BlockSpec((ctx_len, dh), lambda b, h, i: (b, k_col + h // group)),
            pl.BlockSpec((ctx_len, dh), lambda b, h, i: (b, v_col + h // group)),
        ]
        args += [ctx, ctx]
    return pl.pallas_call(
        functools.partial(_attn_kernel, scale=dh ** -0.5, has_ctx=has_ctx),
        grid=(nb, heads, nq),
        in_specs=in_specs,
        out_specs=pl.BlockSpec((tq, dh), lambda b, h, i: (b * nq + i, h)),
        out_shape=jax.ShapeDtypeStruct((nb * seq, heads * dh), BF16),
        compiler_params=_cparams("parallel", "parallel", "arbitrary"),
        name="dense_attention",
    )(*args)


def _natten_plan(rows, qr):
    w = GRID_W
    kr = min(NA_WIN_ROWS, rows)
    wr = min(rows, kr if qr == 1 else qr + NA_WIN_ROWS)
    cols = np.arange(w)
    cstart = np.clip(cols - NA_WIN_COLS // 2, 0, w - NA_WIN_COLS)
    col_in = (cols[None, :] >= cstart[:, None]) & (cols[None, :] < cstart[:, None] + NA_WIN_COLS)
    col_idx = np.clip(cols[None, :] - cols[:, None] + NA_WIN_COLS - 1, 0, 2 * NA_WIN_COLS - 2)
    keys, cases, w0s, case_of = {}, [], [], []
    for blk in range(rows // qr):
        r0 = blk * qr
        w0 = int(np.clip(r0 - kr // 2, 0, rows - wr))
        q_rows = r0 + np.arange(qr)
        rs = np.clip(q_rows - kr // 2, 0, rows - kr)
        k_rows = w0 + np.arange(wr)
        assert rs.min() >= w0 and rs.max() + kr <= w0 + wr
        key = (w0 - r0, tuple((rs - q_rows).tolist()))
        if key not in keys:
            keys[key] = len(cases)
            rvalid = (k_rows[None, :] >= rs[:, None]) & (k_rows[None, :] < rs[:, None] + kr)
            roff = np.clip(k_rows[None, :] - q_rows[:, None] + NA_WIN_ROWS - 1, 0, 2 * NA_WIN_ROWS - 2)
            shape = (qr, w, wr, w)
            ridx = np.broadcast_to(roff[:, None, :, None], shape).reshape(qr * w, wr * w)
            cidx = np.broadcast_to(col_idx[None, :, None, :], shape).reshape(qr * w, wr * w)
            valid = (rvalid[:, None, :, None] & col_in[None, :, None, :]).reshape(qr * w, wr * w)
            cases.append((ridx, cidx, valid))
        w0s.append(w0)
        case_of.append(keys[key])
    return wr, np.asarray(w0s, np.int32), np.asarray(case_of, np.int32), cases


def _natten_kernel(w0_ref, case_ref, q_ref, k_ref, v_ref, kc_ref, vc_ref, tab_ref, o_ref, *, qr, wr, nblk, scale):
    nq, nk = qr * GRID_W, wr * GRID_W
    kc = kc_ref[...]
    vc = vc_ref[...]

    def body(blk, carry):
        q0 = pl.multiple_of(blk * nq, nq)
        k0 = pl.multiple_of(w0_ref[blk] * GRID_W, GRID_W)
        q = q_ref[pl.ds(q0, nq), :]
        s = _dot_nt(q, k_ref[pl.ds(k0, nk), :]) * scale + tab_ref[case_ref[blk]]
        sc = _dot_nt(q, kc) * scale
        m = jnp.maximum(jnp.max(s, axis=-1, keepdims=True), jnp.max(sc, axis=-1, keepdims=True))
        p = jnp.exp(s - m)
        pc = jnp.exp(sc - m)
        l = jnp.sum(p, axis=-1, keepdims=True) + jnp.sum(pc, axis=-1, keepdims=True)
        o = _dot(p.astype(BF16), v_ref[pl.ds(k0, nk), :]) + _dot(pc.astype(BF16), vc)
        o_ref[pl.ds(q0, nq), :] = (o / l).astype(o_ref.dtype)
        return carry

    lax.fori_loop(0, nblk, body, 0)


def _natten(qkv, qkv_ctx, rpb, nb, seq, ctx_len, *, qr):
    heads, dh = NA_HEADS, HEAD_DIM
    rows = seq // GRID_W
    wr, w0s, case_of, cases = _natten_plan(rows, qr)
    nblk = rows // qr
    tab = jnp.stack([jnp.where(valid[None], rpb[:, ridx, cidx], NEG_INF) for ridx, cidx, valid in cases], axis=1)
    ncase = len(cases)
    grid_spec = pltpu.PrefetchScalarGridSpec(
        num_scalar_prefetch=2,
        grid=(heads, nb),
        in_specs=[
            pl.BlockSpec((seq, dh), lambda h, b, *_: (b, h)),
            pl.BlockSpec((seq, dh), lambda h, b, *_: (b, heads + h)),
            pl.BlockSpec((seq, dh), lambda h, b, *_: (b, 2 * heads + h)),
            pl.BlockSpec((ctx_len, dh), lambda h, b, *_: (b, heads + h)),
            pl.BlockSpec((ctx_len, dh), lambda h, b, *_: (b, 2 * heads + h)),
            pl.BlockSpec((None, ncase, qr * GRID_W, wr * GRID_W), lambda h, b, *_: (h, 0, 0, 0)),
        ],
        out_specs=pl.BlockSpec((seq, dh), lambda h, b, *_: (b, h)),
    )
    return pl.pallas_call(
        functools.partial(_natten_kernel, qr=qr, wr=wr, nblk=nblk, scale=dh ** -0.5),
        grid_spec=grid_spec,
        out_shape=jax.ShapeDtypeStruct((nb * seq, heads * dh), BF16),
        compiler_params=_cparams("parallel", "arbitrary"),
        name="natten",
    )(jnp.asarray(w0s), jnp.asarray(case_of), qkv, qkv, qkv, qkv_ctx, qkv_ctx, tab)


def _rope_tables(seq):
    t = jnp.arange(seq)
    quarter = HEAD_DIM // 4
    inv = ROPE_THETA ** (-jnp.arange(quarter, dtype=F32) / quarter)
    cos, sin = [], []
    for pos in (t // GRID_W, t % GRID_W):
        ang = pos.astype(F32)[:, None] * inv[None, :]
        cos += [jnp.cos(ang), jnp.cos(ang)]
        sin += [-jnp.sin(ang), jnp.sin(ang)]
    return jnp.concatenate(cos, axis=-1), jnp.concatenate(sin, axis=-1)


def _gqa_proj_kernel(*refs, rope, n_q, n_kv):
    if rope:
        x_ref, mod_ref, g_ref, w_ref, gq_ref, gk_ref, cos_ref, sin_ref, o_ref = refs
    else:
        x_ref, mod_ref, g_ref, w_ref, gq_ref, gk_ref, o_ref = refs
    dh = HEAD_DIM
    xn = _adanorm(x_ref[...], g_ref[...], mod_ref, 1).astype(BF16)
    if rope:
        cos = cos_ref[...]
        sin = sin_ref[...]
        lane = lax.broadcasted_iota(jnp.int32, cos.shape, 1)
        low_quarter = (lane % (dh // 2)) < (dh // 4)
    grp = 4
    for c in range((n_q + 2 * n_kv) // grp):
        y = _dot(xn, w_ref[:, c * grp * dh:(c + 1) * grp * dh])
        for hh in range(grp):
            head = c * grp + hh
            yh = y[:, hh * dh:(hh + 1) * dh]
            if head < n_q + n_kv:
                yh = _rms(yh, gq_ref[...] if head < n_q else gk_ref[...])
                if rope:
                    partner = jnp.where(low_quarter, pltpu.roll(yh, dh - dh // 4, 1), pltpu.roll(yh, dh // 4, 1))
                    yh = yh * cos + partner * sin
            o_ref[:, head * dh:(head + 1) * dh] = yh.astype(o_ref.dtype)


def _gqa_proj(x, mod, g, w, gq, gk, rope_tabs, *, tps, tm):
    n, d = x.shape
    n_out = w.shape[1]
    rope = rope_tabs is not None
    in_specs = [
        pl.BlockSpec((tm, d), lambda i: (i, 0)),
        pl.BlockSpec((None, 9, d), lambda i: (i // tps, 0, 0)),
        pl.BlockSpec((1, d), lambda i: (0, 0)),
        pl.BlockSpec((d, n_out), lambda i: (0, 0)),
        pl.BlockSpec((1, HEAD_DIM), lambda i: (0, 0)),
        pl.BlockSpec((1, HEAD_DIM), lambda i: (0, 0)),
    ]
    args = [x, mod, g.reshape(1, d), w, gq.reshape(1, -1), gk.reshape(1, -1)]
    if rope:
        in_specs += [pl.BlockSpec((tm, HEAD_DIM), lambda i: (i % tps, 0))] * 2
        args += list(rope_tabs)
    return pl.pallas_call(
        functools.partial(_gqa_proj_kernel, rope=rope, n_q=GA_HEADS, n_kv=GA_KV_HEADS),
        grid=(n // tm,),
        in_specs=in_specs,
        out_specs=pl.BlockSpec((tm, n_out), lambda i: (i, 0)),
        out_shape=jax.ShapeDtypeStruct((n, n_out), BF16),
        compiler_params=_cparams("parallel"),
        name="gqa_proj",
    )(*args)


def _conv_out_kernel(zp_ref, zc_ref, zn_ref, wdw_ref, bdw_ref, lg_ref, lb_ref, w2_ref, b2_ref, x_ref, mod_ref,
                     o_ref, ext_ref, cv_ref, *, tps, tm):
    d = zc_ref.shape[1]
    c = pl.program_id(0) % tps
    half = CV_KERNEL // 2
    ext_ref[0:CONV_HALO, :] = jnp.where(c > 0, zp_ref[...], 0.0)
    ext_ref[CONV_HALO:CONV_HALO + tm, :] = zc_ref[...]
    ext_ref[CONV_HALO + tm:, :] = jnp.where(c < tps - 1, zn_ref[...], 0.0)
    rb, cb = 64, 512
    for r0 in range(0, tm, rb):
        for c0 in range(0, d, cb):
            acc = jnp.zeros((rb, cb), F32)
            for k in range(CV_KERNEL):
                off = CONV_HALO - half + r0 + k
                acc = acc + wdw_ref[k:k + 1, c0:c0 + cb] * ext_ref[off:off + rb, c0:c0 + cb]
            cv_ref[r0:r0 + rb, c0:c0 + cb] = acc + bdw_ref[:, c0:c0 + cb]
    z = cv_ref[...]
    mu = jnp.mean(z, axis=-1, keepdims=True)
    zc = z - mu
    var = jnp.mean(zc * zc, axis=-1, keepdims=True)
    zn = zc * lax.rsqrt(var + EPS) * lg_ref[...] + lb_ref[...]
    a = (zn * _sigmoid(zn)).astype(BF16)
    y = _dot(a, w2_ref[...]) + b2_ref[...]
    o_ref[...] = x_ref[...] + _gate_row(mod_ref, 1) * y


def _conv_out(z, wdw, bdw, ln_g, ln_b, w2, b2, x, mod, *, tps, mod_tps, tm):
    n, d = x.shape
    hb = tm // CONV_HALO
    last_hb = n // CONV_HALO - 1
    row = lambda v: v.reshape(1, d)
    return pl.pallas_call(
        functools.partial(_conv_out_kernel, tps=tps, tm=tm),
        grid=(n // tm,),
        in_specs=[
            pl.BlockSpec((CONV_HALO, d), lambda i: (jnp.maximum(i * hb - 1, 0), 0)),
            pl.BlockSpec((tm, d), lambda i: (i, 0)),
            pl.BlockSpec((CONV_HALO, d), lambda i: (jnp.minimum((i + 1) * hb, last_hb), 0)),
            pl.BlockSpec((CV_KERNEL, d), lambda i: (0, 0)),
            pl.BlockSpec((1, d), lambda i: (0, 0)),
            pl.BlockSpec((1, d), lambda i: (0, 0)),
            pl.BlockSpec((1, d), lambda i: (0, 0)),
            pl.BlockSpec((d, d), lambda i: (0, 0)),
            pl.BlockSpec((1, d), lambda i: (0, 0)),
            pl.BlockSpec((tm, d), lambda i: (i, 0)),
            pl.BlockSpec((None, 9, d), lambda i: (i // mod_tps, 0, 0)),
        ],
        out_specs=pl.BlockSpec((tm, d), lambda i: (i, 0)),
        out_shape=jax.ShapeDtypeStruct((n, d), F32),
        scratch_shapes=[pltpu.VMEM((tm + 2 * CONV_HALO, d), F32), pltpu.VMEM((tm, d), F32)],
        compiler_params=_cparams("parallel"),
        name="conv_module_tail",
    )(z, z, z, wdw, row(bdw), row(ln_g), row(ln_b), w2, row(b2), x, mod)


def _softplus(x):
    return jnp.maximum(x, 0.0) + jnp.log1p(jnp.exp(-jnp.abs(x)))


def _lru_kernel(*refs, rev, tt, nt, add_prev):
    refs = list(refs)
    halo_ref, u_ref, cw_ref, cb_ref, wa_ref, ba_ref, wi_ref, bi_ref, lam_ref, h0_ref = refs[:10]
    refs = refs[10:]
    yp_ref = refs.pop(0) if add_prev else None
    y_ref, hl_ref, ext_ref, a_ref, b_ref, h_ref = refs
    i = pl.program_id(1)
    first = i == 0
    blk = wa_ref.shape[1]

    @pl.when(first)
    def _():
        h_ref[...] = h0_ref[...]

    halo = jnp.where(first, 0.0, halo_ref[...])
    if rev:
        ext_ref[0:tt, :] = u_ref[...]
        ext_ref[tt:tt + LRU_HALO, :] = halo
        offs = [LRU_CONV - 1 - k for k in range(LRU_CONV)]
    else:
        ext_ref[0:LRU_HALO, :] = halo
        ext_ref[LRU_HALO:LRU_HALO + tt, :] = u_ref[...]
        offs = [LRU_HALO - (LRU_CONV - 1) + k for k in range(LRU_CONV)]
    sp = _softplus(-lam_ref[...])
    for n in range(LRU_BLOCKS):
        cs = slice(n * blk, (n + 1) * blk)
        ud = cb_ref[:, cs] + cw_ref[0:1, cs] * ext_ref[offs[0]:offs[0] + tt, cs]
        for k in range(1, LRU_CONV):
            ud = ud + cw_ref[k:k + 1, cs] * ext_ref[offs[k]:offs[k] + tt, cs]
        zb = ud.astype(BF16)
        r = _sigmoid(_dot(zb, wa_ref[n]) + ba_ref[:, cs])
        ig = _sigmoid(_dot(zb, wi_ref[n]) + bi_ref[:, cs])
        log_a = (-LRU_C * r) * sp[:, cs]
        th = jnp.tanh(log_a)
        a_ref[:, cs] = jnp.exp(log_a)
        b_ref[:, cs] = jnp.sqrt((-2.0 * th) / (1.0 - th)) * (ig * ud)

    ngrp = tt // SUBLANES

    def group(gi, h):
        g = (ngrp - 1 - gi) if rev else gi
        base = pl.multiple_of(g * SUBLANES, SUBLANES)
        a8 = a_ref.at[pl.ds(base, SUBLANES)]
        b8 = b_ref.at[pl.ds(base, SUBLANES)]
        y8 = y_ref.at[pl.ds(base, SUBLANES)]
        order = range(SUBLANES - 1, -1, -1) if rev else range(SUBLANES)
        for j in order:
            h = a8[j:j + 1, :] * h + b8[j:j + 1, :]
            if add_prev:
                y8[j:j + 1, :] = h + yp_ref.at[pl.ds(base, SUBLANES)][j:j + 1, :]
            else:
                y8[j:j + 1, :] = h
        return h

    h = lax.fori_loop(0, ngrp, group, h_ref[...])
    h_ref[...] = h

    @pl.when(i == nt - 1)
    def _():
        hl_ref[...] = h


def _lru_scan(u, conv_w, conv_b, wa, ba, wi, bi, lam, h0, y_prev, nb, seq, *, rev, tt):
    r = wa.shape[0] * wa.shape[1]
    nt = seq // tt
    hb = tt // LRU_HALO
    last_hb = (nb * seq) // LRU_HALO - 1
    tile = (lambda b, i: b * nt + (nt - 1 - i)) if rev else (lambda b, i: b * nt + i)
    if rev:
        halo_map = lambda b, i: (jnp.minimum((tile(b, i) + 1) * hb, last_hb), 0)
    else:
        halo_map = lambda b, i: (jnp.maximum(tile(b, i) * hb - 1, 0), 0)
    row = lambda v: v.reshape(1, r)
    vec = lambda: pl.BlockSpec((1, r), lambda b, i: (0, 0))
    in_specs = [
        pl.BlockSpec((LRU_HALO, r), halo_map),
        pl.BlockSpec((tt, r), lambda b, i: (tile(b, i), 0)),
        pl.BlockSpec((LRU_CONV, r), lambda b, i: (0, 0)),
        vec(),
        pl.BlockSpec(wa.shape, lambda b, i: (0, 0, 0)),
        vec(),
        pl.BlockSpec(wi.shape, lambda b, i: (0, 0, 0)),
        vec(),
        vec(),
        pl.BlockSpec((None, 1, r), lambda b, i: (b, 0, 0)),
    ]
    args = [u, u, conv_w, row(conv_b), wa, row(ba), wi, row(bi), row(lam), h0]
    add_prev = y_prev is not None
    if add_prev:
        in_specs.append(pl.BlockSpec((tt, r), lambda b, i: (tile(b, i), 0)))
        args.append(y_prev)
    return pl.pallas_call(
        functools.partial(_lru_kernel, rev=rev, tt=tt, nt=nt, add_prev=add_prev),
        grid=(nb, nt),
        in_specs=in_specs,
        out_specs=[
            pl.BlockSpec((tt, r), lambda b, i: (tile(b, i), 0)),
            pl.BlockSpec((None, 1, r), lambda b, i: (b, 0, 0)),
        ],
        out_shape=[jax.ShapeDtypeStruct((nb * seq, r), F32), jax.ShapeDtypeStruct((nb, 1, r), F32)],
        scratch_shapes=[
            pltpu.VMEM((tt + LRU_HALO, r), F32),
            pltpu.VMEM((tt, r), F32),
            pltpu.VMEM((tt, r), F32),
            pltpu.VMEM((1, r), F32),
        ],
        compiler_params=_cparams("parallel", "arbitrary"),
        name="rglru_scan",
    )(*args)


def _tile(n, pref):
    t = min(pref, n)
    while n % t:
        t //= 2
    return t


def kernel(x, c, ctx, c_ctx, ada_w, ada_b, norm_g, ffn_w13, ffn_w2, final_g, na_wqkv, na_wo, na_rpb, cv_w1, cv_b1, cv_wdw, cv_bdw, cv_ln_g, cv_ln_b, cv_w2, cv_b2, ga_wq, ga_wkv, ga_wo, ga_gq, ga_gk, lru_win, lru_convw, lru_convb, lru_wa, lru_ba, lru_wi, lru_bi, lru_lam, lru_wout):
    nb, seq, d = x.shape
    ctx_len = ctx.shape[1]
    depth = ada_w.shape[0]
    n_lat, n_ctx = nb * seq, nb * ctx_len
    xs = x.reshape(n_lat, d)
    hs = ctx.reshape(n_ctx, d)

    seg = -(-(nb + 1) // SUBLANES) * SUBLANES
    cvec = jnp.zeros((seg, d), F32).at[:nb].set(c).at[nb].set(c_ctx)
    mod = _ada_mod(cvec, ada_w, ada_b)

    w13_b, w2_b = ffn_w13.astype(BF16), ffn_w2.astype(BF16)
    ffn_hidden = ffn_w2.shape[2]

    tm_l = _tile(seq, 512)
    tm_c = _tile(n_ctx, 512)
    tf = 512 if ffn_hidden % 512 == 0 else 128
    tps_l = seq // tm_l
    tps_c = n_ctx // tm_c

    def ffn(v, mod_v, i, half, sub, tps, tm, fin=None):
        return _swiglu(v, mod_v, norm_g[i, sub], w13_b[i, half], w2_b[i, half], sub=sub, tps=tps, tm=tm, tf=tf,
                       final_g=fin)

    for i in range(depth):
        kind, j = i % N_MIXERS, i // N_MIXERS
        last = i == depth - 1
        ctx_in = not (last and kind == 1)
        ctx_out = not last
        mod_l, mod_c = mod[i, :nb], mod[i, nb:nb + 1]
        g1 = norm_g[i, 1]

        xs = ffn(xs, mod_l, i, 0, 0, tps_l, tm_l)
        if ctx_in:
            hs = ffn(hs, mod_c, i, 0, 0, tps_c, tm_c)

        if kind == 0:
            wqkv = na_wqkv[j].astype(BF16)
            wo = na_wo[j].astype(BF16)
            pj = functools.partial(_proj, g=g1, w=wqkv, n_out=3 * d, out_dtype=BF16, tn=1024)
            qkv_l = pj(xs, mod_l, tps=tps_l, tm=tm_l)
            qkv_c = pj(hs, mod_c, tps=tps_c, tm=tm_c)
            rows = seq // GRID_W
            qr = 8 if rows % 8 == 0 and rows >= 16 else 1
            o_l = _natten(qkv_l, qkv_c, na_rpb[j], nb, seq, ctx_len, qr=qr)
            a_spec = lambda tm: [pl.BlockSpec((tm, d), lambda t: (t, 0))]
            xs = _outres([o_l], a_spec(tm_l), wo, None, xs, mod_l, mode="plain", tps=tps_l, tm=tm_l)
            if ctx_out:
                o_c = _attention(qkv_c, ctx_len, nb, heads=NA_HEADS, group=1, k_col=NA_HEADS, v_col=2 * NA_HEADS,
                                 tq=ctx_len)
                hs = _outres([o_c], a_spec(tm_c), wo, None, hs, mod_c, mode="plain", tps=tps_c, tm=tm_c)
        elif kind == 1:
            w1 = cv_w1[j].astype(BF16)
            w2 = cv_w2[j].astype(BF16)
            pj = functools.partial(_proj, g=g1, w=w1, bias=cv_b1[j], n_out=d, glu=True, out_dtype=F32, tn=1024)
            tail = functools.partial(_conv_out, wdw=cv_wdw[j], bdw=cv_bdw[j], ln_g=cv_ln_g[j], ln_b=cv_ln_b[j],
                                     w2=w2, b2=cv_b2[j])
            tc_l, tc_c = _tile(seq, 256), _tile(ctx_len, 256)
            z_l = pj(xs, mod_l, tps=tps_l, tm=tm_l)
            xs = tail(z_l, x=xs, mod=mod_l, tps=seq // tc_l, mod_tps=seq // tc_l, tm=tc_l)
            if ctx_out:
                z_c = pj(hs, mod_c, tps=tps_c, tm=tm_c)
                hs = tail(z_c, x=hs, mod=mod_c, tps=ctx_len // tc_c, mod_tps=n_ctx // tc_c, tm=tc_c)
        elif kind == 2:
            wqkv = jnp.concatenate([ga_wq[j], ga_wkv[j]], axis=1).astype(BF16)
            wo = ga_wo[j].astype(BF16)
            nqc = GA_HEADS
            p_l = _gqa_proj(xs, mod_l, g1, wqkv, ga_gq[j], ga_gk[j], _rope_tables(seq), tps=tps_l, tm=tm_l)
            p_c = _gqa_proj(hs, mod_c, g1, wqkv, ga_gq[j], ga_gk[j], None, tps=tps_c, tm=tm_c)
            grp = GA_HEADS // GA_KV_HEADS
            o_l = _attention(p_l, seq, nb, heads=GA_HEADS, group=grp, k_col=nqc, v_col=nqc + GA_KV_HEADS,
                             tq=_tile(seq, 512), ctx=p_c, ctx_len=ctx_len)
            a_spec = lambda tm: [pl.BlockSpec((tm, d), lambda t: (t, 0))]
            xs = _outres([o_l], a_spec(tm_l), wo, None, xs, mod_l, mode="plain", tps=tps_l, tm=tm_l)
            if ctx_out:
                o_c = _attention(p_c, ctx_len, nb, heads=GA_HEADS, group=grp, k_col=nqc, v_col=nqc + GA_KV_HEADS,
                                 tq=ctx_len)
                hs = _outres([o_c], a_spec(tm_c), wo, None, hs, mod_c, mode="plain", tps=tps_c, tm=tm_c)
        else:
            if ctx_out:
                raise NotImplementedError("RG-LRU context output is never needed: the LRU layer is the last one")
            r = lru_win.shape[2] // 2
            w_in = lru_win[j].astype(BF16)
            w_out = lru_wout[j].astype(BF16)
            wa, wi = lru_wa[j].astype(BF16), lru_wi[j].astype(BF16)
            u_c = _proj(hs, mod_c, g1, w_in, n_out=r, out_dtype=F32, tps=tps_c, tm=tm_c, tn=512)
            u_l = _proj(xs, mod_l, g1, w_in, n_out=2 * r, out_dtype=F32, tps=tps_l, tm=tm_l, tn=512)
            tt_l, tt_c = _tile(seq, 256), _tile(ctx_len, 256)
            zeros = jnp.zeros((nb, 1, r), F32)
            y = None
            for e in range(2):
                prm = (lru_convw[j, e], lru_convb[j, e], wa[e], lru_ba[j, e], wi[e], lru_bi[j, e], lru_lam[j, e])
                _, h_ctx = _lru_scan(u_c, *prm, zeros, None, nb, ctx_len, rev=bool(e), tt=tt_c)
                y, _ = _lru_scan(u_l, *prm, h_ctx, y, nb, seq, rev=bool(e), tt=tt_l)
            tm_o = _tile(seq, 256)
            a_specs = [pl.BlockSpec((tm_o, r), lambda t: (t, 1)), pl.BlockSpec((tm_o, r), lambda t: (t, 0))]
            xs = _outres([u_l, y], a_specs, w_out, None, xs, mod_l, mode="gelu_mul", tps=seq // tm_o, tm=tm_o)

        xs = ffn(xs, mod_l, i, 1, 2, tps_l, tm_l, fin=final_g if last else None)
        if ctx_out:
            hs = ffn(hs, mod_c, i, 1, 2, tps_c, tm_c)
    return xs.reshape(nb, seq, d)
```

```python
import functools

import numpy as np
import jax
import jax.numpy as jnp
from jax import lax
from jax.experimental import pallas as pl
from jax.experimental.pallas import tpu as pltpu

F32 = jnp.float32
BF16 = jnp.bfloat16

GRID_W = 64
EPS = 1e-6
NEG_INF = -1e30
HEAD_DIM = 128
NA_HEADS = 16
NA_WIN_ROWS = 8
NA_WIN_COLS = 16
CV_KERNEL = 31
GA_HEADS = 16
GA_KV_HEADS = 4
ROPE_THETA = 10000.0
LRU_BLOCKS = 10
LRU_CONV = 4
LRU_C = 8.0
N_MIXERS = 4

V7X_VMEM_LIMIT_BYTES = 56 * 1024 * 1024
SUBLANES = 8
CONV_HALO = 16
LRU_HALO = 8
ATTN_KEY_CHUNK = 512
LOG2E = 1.4426950408889634


def _cparams(*sem):
    return pltpu.CompilerParams(dimension_semantics=sem, vmem_limit_bytes=V7X_VMEM_LIMIT_BYTES)


def _dot(a, b):
    return jnp.dot(a, b, preferred_element_type=F32)


def _dot_nt(a, b):
    return lax.dot_general(a, b, (((1,), (1,)), ((), ())), preferred_element_type=F32)


def _sigmoid(x):
    return jax.nn.sigmoid(x)


def _rms(x, g):
    return x * lax.rsqrt(jnp.mean(x * x, axis=-1, keepdims=True) + EPS) * g


def _adanorm(x, g, mod_ref, sub):
    shift = mod_ref[3 * sub:3 * sub + 1, :]
    scale = mod_ref[3 * sub + 1:3 * sub + 2, :]
    return _rms(x, g) * (1.0 + scale) + shift


def _gate_row(mod_ref, sub):
    return mod_ref[3 * sub + 2:3 * sub + 3, :]


def _ada_kernel(c_ref, w_ref, b_ref, o_ref):
    c = c_ref[...]
    s = (c * _sigmoid(c)).astype(BF16)
    o_ref[...] = _dot(s, w_ref[...].astype(BF16)) + b_ref[...]


def _ada_mod(cvec, ada_w, ada_b):
    depth, d, _ = ada_w.shape
    s = cvec.shape[0]
    out = pl.pallas_call(
        _ada_kernel,
        grid=(depth, 9),
        in_specs=[
            pl.BlockSpec((s, d), lambda i, j: (0, 0)),
            pl.BlockSpec((None, d, d), lambda i, j: (i, 0, j)),
            pl.BlockSpec((None, 1, d), lambda i, j: (i, 0, j)),
        ],
        out_specs=pl.BlockSpec((None, None, s, d), lambda i, j: (i, j, 0, 0)),
        out_shape=jax.ShapeDtypeStruct((depth, 9, s, d), F32),
        compiler_params=_cparams("parallel", "arbitrary"),
        name="ada_mod",
    )(cvec, ada_w, ada_b.reshape(depth, 1, 9 * d))
    return jnp.transpose(out, (0, 2, 1, 3))


def _swiglu_kernel(*refs, sub, final, nf):
    if final:
        x_ref, mod_ref, g_ref, w1_ref, w3_ref, w2_ref, fg_ref, o_ref, xn_ref, acc_ref = refs
    else:
        x_ref, mod_ref, g_ref, w1_ref, w3_ref, w2_ref, o_ref, xn_ref, acc_ref = refs
    j = pl.program_id(1)

    @pl.when(j == 0)
    def _():
        xn_ref[...] = _adanorm(x_ref[...], g_ref[...], mod_ref, sub).astype(BF16)
        acc_ref[...] = jnp.zeros_like(acc_ref)

    xn = xn_ref[...]
    gt = _dot(xn, w1_ref[...])
    up = _dot(xn, w3_ref[...])
    hid = (gt * _sigmoid(gt)) * up
    acc_ref[...] += _dot(hid.astype(BF16), w2_ref[...])

    @pl.when(j == nf - 1)
    def _():
        y = x_ref[...] + (0.5 * _gate_row(mod_ref, sub)) * acc_ref[...]
        if final:
            y = _rms(y, fg_ref[...])
        o_ref[...] = y


def _swiglu(x, mod, g, w13, w2, layer, half, *, sub, tps, tm, tf, final_g=None):
    n, d = x.shape
    f = w2.shape[2]
    nf = f // tf
    final = final_g is not None
    in_specs = [
        pl.BlockSpec((tm, d), lambda i, j: (i, 0)),
        pl.BlockSpec((None, 9, d), lambda i, j: (i // tps, 0, 0)),
        pl.BlockSpec((1, d), lambda i, j: (0, 0)),
        pl.BlockSpec((None, None, d, tf), lambda i, j: (layer, half, 0, j)),
        pl.BlockSpec((None, None, d, tf), lambda i, j: (layer, half, 0, j + nf)),
        pl.BlockSpec((None, None, tf, d), lambda i, j: (layer, half, j, 0)),
    ]
    args = [x, mod, g.reshape(1, d), w13, w13, w2]
    if final:
        in_specs.append(pl.BlockSpec((1, d), lambda i, j: (0, 0)))
        args.append(final_g.reshape(1, d))
    return pl.pallas_call(
        functools.partial(_swiglu_kernel, sub=sub, final=final, nf=nf),
        grid=(n // tm, nf),
        in_specs=in_specs,
        out_specs=pl.BlockSpec((tm, d), lambda i, j: (i, 0)),
        out_shape=jax.ShapeDtypeStruct((n, d), F32),
        scratch_shapes=[pltpu.VMEM((tm, d), BF16), pltpu.VMEM((tm, d), F32)],
        compiler_params=_cparams("parallel", "arbitrary"),
        name="swiglu",
    )(*args)


def _proj_kernel(*refs, sub, glu, has_bias):
    x_ref, mod_ref, g_ref = refs[:3]
    rest = list(refs[3:])
    wa_ref = rest.pop(0)
    wg_ref = rest.pop(0) if glu else None
    ba_ref = rest.pop(0) if has_bias else None
    bg_ref = rest.pop(0) if (has_bias and glu) else None
    o_ref, xn_ref = rest

    @pl.when(pl.program_id(1) == 0)
    def _():
        xn_ref[...] = _adanorm(x_ref[...], g_ref[...], mod_ref, sub).astype(BF16)

    xn = xn_ref[...]
    y = _dot(xn, wa_ref[...])
    if has_bias:
        y = y + ba_ref[...]
    if glu:
        gt = _dot(xn, wg_ref[...])
        if has_bias:
            gt = gt + bg_ref[...]
        y = y * _sigmoid(gt)
    o_ref[...] = y.astype(o_ref.dtype)


def _proj(x, mod, g, w, bias=None, *, n_out, glu=False, out_dtype, tps, tm, tn):
    n, d = x.shape
    nj = n_out // tn
    has_bias = bias is not None
    in_specs = [
        pl.BlockSpec((tm, d), lambda i, j: (i, 0)),
        pl.BlockSpec((None, 9, d), lambda i, j: (i // tps, 0, 0)),
        pl.BlockSpec((1, d), lambda i, j: (0, 0)),
        pl.BlockSpec((d, tn), lambda i, j: (0, j)),
    ]
    args = [x, mod, g.reshape(1, d), w]
    if glu:
        in_specs.append(pl.BlockSpec((d, tn), lambda i, j: (0, j + nj)))
        args.append(w)
    if has_bias:
        b2 = bias.reshape(1, -1)
        in_specs.append(pl.BlockSpec((1, tn), lambda i, j: (0, j)))
        args.append(b2)
        if glu:
            in_specs.append(pl.BlockSpec((1, tn), lambda i, j: (0, j + nj)))
            args.append(b2)
    return pl.pallas_call(
        functools.partial(_proj_kernel, sub=1, glu=glu, has_bias=has_bias),
        grid=(n // tm, nj),
        in_specs=in_specs,
        out_specs=pl.BlockSpec((tm, tn), lambda i, j: (i, j)),
        out_shape=jax.ShapeDtypeStruct((n, n_out), out_dtype),
        scratch_shapes=[pltpu.VMEM((tm, d), BF16)],
        compiler_params=_cparams("parallel", "arbitrary"),
        name="adanorm_proj",
    )(*args)


def _outres_kernel(*refs, mode, has_bias):
    refs = list(refs)
    if mode == "gelu_mul":
        ug_ref, yr_ref = refs.pop(0), refs.pop(0)
        a = (jax.nn.gelu(ug_ref[...], approximate=True) * yr_ref[...]).astype(BF16)
    else:
        a = refs.pop(0)[...]
    w_ref = refs.pop(0)
    b_ref = refs.pop(0) if has_bias else None
    x_ref, mod_ref, o_ref = refs
    y = _dot(a, w_ref[...])
    if has_bias:
        y = y + b_ref[...]
    o_ref[...] = x_ref[...] + _gate_row(mod_ref, 1) * y


def _outres(a_list, a_specs, w, bias, x, mod, *, mode, tps, tm):
    n, d = x.shape
    k = w.shape[0]
    has_bias = bias is not None
    in_specs = list(a_specs) + [pl.BlockSpec((k, d), lambda i: (0, 0))]
    args = list(a_list) + [w]
    if has_bias:
        in_specs.append(pl.BlockSpec((1, d), lambda i: (0, 0)))
        args.append(bias.reshape(1, d))
    in_specs += [
        pl.BlockSpec((tm, d), lambda i: (i, 0)),
        pl.BlockSpec((None, 9, d), lambda i: (i // tps, 0, 0)),
    ]
    args += [x, mod]
    return pl.pallas_call(
        functools.partial(_outres_kernel, mode=mode, has_bias=has_bias),
        grid=(n // tm,),
        in_specs=in_specs,
        out_specs=pl.BlockSpec((tm, d), lambda i: (i, 0)),
        out_shape=jax.ShapeDtypeStruct((n, d), F32),
        compiler_params=_cparams("parallel"),
        name="mixer_out_residual",
    )(*args)


def _online_softmax_step(state, s2, v):
    m, l, acc = state
    mn = jnp.maximum(m, jnp.max(s2, axis=-1, keepdims=True))
    alpha = jnp.exp2(m - mn)
    p = jnp.exp2(s2 - mn)
    l = alpha * l + jnp.sum(p, axis=-1, keepdims=True)
    acc = alpha * acc + _dot(p.astype(BF16), v)
    return mn, l, acc


def _softmax_init(nq):
    return jnp.full((nq, 1), -jnp.inf, F32), jnp.zeros((nq, 1), F32), jnp.zeros((nq, HEAD_DIM), F32)


def _attn_kernel(*refs, scale, has_ctx, kch):
    if has_ctx:
        q_ref, k_ref, v_ref, kc_ref, vc_ref, o_ref = refs
    else:
        q_ref, k_ref, v_ref, o_ref = refs
    q = q_ref[...]
    c2 = scale * LOG2E
    state = _softmax_init(q.shape[0])
    for c0 in range(0, k_ref.shape[0], kch):
        state = _online_softmax_step(state, _dot_nt(q, k_ref[c0:c0 + kch, :]) * c2, v_ref[c0:c0 + kch, :])
    if has_ctx:
        state = _online_softmax_step(state, _dot_nt(q, kc_ref[...]) * c2, vc_ref[...])
    _, l, acc = state
    o_ref[...] = (acc / l).astype(o_ref.dtype)


def _attention(qkv, seq, nb, *, heads, group, k_col, v_col, tq, ctx=None, ctx_len=None):
    dh = HEAD_DIM
    nq = seq // tq
    has_ctx = ctx is not None
    in_specs = [
        pl.BlockSpec((tq, dh), lambda b, h, i: (b * nq + i, h)),
        pl.BlockSpec((seq, dh), lambda b, h, i: (b, k_col + h // group)),
        pl.BlockSpec((seq, dh), lambda b, h, i: (b, v_col + h // group)),
    ]
    args = [qkv, qkv, qkv]
    if has_ctx:
        in_specs += [
            pl.BlockSpec((ctx_len, dh), lambda b, h, i: (b, k_col + h // group)),
            pl.BlockSpec((ctx_len, dh), lambda b, h, i: (b, v_col + h // group)),
        ]
        args += [ctx, ctx]
    return pl.pallas_call(
        functools.partial(_attn_kernel, scale=dh ** -0.5, has_ctx=has_ctx, kch=_tile(seq, ATTN_KEY_CHUNK)),
        grid=(nb, heads, nq),
        in_specs=in_specs,
        out_specs=pl.BlockSpec((tq, dh), lambda b, h, i: (b * nq + i, h)),
        out_shape=jax.ShapeDtypeStruct((nb * seq, heads * dh), BF16),
        compiler_params=_cparams("parallel", "parallel", "arbitrary"),
        name="dense_attention",
    )(*args)


def _natten_plan(rows, qr):
    w = GRID_W
    kr = min(NA_WIN_ROWS, rows)
    wr = min(rows, kr if qr == 1 else qr + NA_WIN_ROWS)
    cols = np.arange(w)
    cstart = np.clip(cols - NA_WIN_COLS // 2, 0, w - NA_WIN_COLS)
    col_in = (cols[None, :] >= cstart[:, None]) & (cols[None, :] < cstart[:, None] + NA_WIN_COLS)
    keys, cases, w0s, case_of = {}, [], [], []
    for blk in range(rows // qr):
        r0 = blk * qr
        w0 = int(np.clip(r0 - kr // 2, 0, rows - wr))
        q_rows = r0 + np.arange(qr)
        rs = np.clip(q_rows - kr // 2, 0, rows - kr)
        k_rows = w0 + np.arange(wr)
        assert rs.min() >= w0 and rs.max() + kr <= w0 + wr
        key = (w0 - r0, tuple((rs - q_rows).tolist()))
        if key not in keys:
            keys[key] = len(cases)
            rvalid = (k_rows[None, :] >= rs[:, None]) & (k_rows[None, :] < rs[:, None] + kr)
            roff = np.clip(k_rows[None, :] - q_rows[:, None] + NA_WIN_ROWS - 1, 0, 2 * NA_WIN_ROWS - 2)
            valid = (rvalid[:, None, :, None] & col_in[None, :, None, :]).reshape(qr * w, wr * w)
            cases.append((roff, valid))
        w0s.append(w0)
        case_of.append(keys[key])
    return wr, np.asarray(w0s, np.int32), np.asarray(case_of, np.int32), cases


def _natten_tables(rpb, cases, qr, wr):
    w = GRID_W
    heads = rpb.shape[0]
    padded = jnp.pad(rpb, ((0, 0), (0, 0), (w - NA_WIN_COLS, w - NA_WIN_COLS)))
    by_col = jnp.stack([padded[:, :, w - 1 - qc:2 * w - 1 - qc] for qc in range(w)], axis=2)
    tabs = []
    for roff, valid in cases:
        t = jnp.take(by_col, jnp.asarray(roff.reshape(-1)), axis=1).reshape(heads, qr, wr, w, w)
        t = jnp.transpose(t, (0, 1, 3, 2, 4)).reshape(heads, qr * w, wr * w)
        tabs.append(jnp.where(valid[None], t * LOG2E, NEG_INF))
    return jnp.stack(tabs, axis=1)


def _natten_kernel(w0_ref, case_ref, q_ref, k_ref, v_ref, kc_ref, vc_ref, tab_ref, o_ref, *, qr, wr, nblk, scale, kch):
    nq, nk = qr * GRID_W, wr * GRID_W
    kc = kc_ref[...]
    vc = vc_ref[...]
    c2 = scale * LOG2E
    unroll = 2 if nblk % 2 == 0 else 1

    def body(it, carry):
        for u in range(unroll):
            blk = it * unroll + u
            q0 = pl.multiple_of(blk * nq, nq)
            k0 = pl.multiple_of(w0_ref[blk] * GRID_W, GRID_W)
            case = case_ref[blk]
            q = q_ref[pl.ds(q0, nq), :]
            state = _softmax_init(nq)
            for c0 in range(0, nk, kch):
                s2 = _dot_nt(q, k_ref[pl.ds(k0 + c0, kch), :]) * c2 + tab_ref[case, :, c0:c0 + kch]
                state = _online_softmax_step(state, s2, v_ref[pl.ds(k0 + c0, kch), :])
            _, l, acc = _online_softmax_step(state, _dot_nt(q, kc) * c2, vc)
            o_ref[pl.ds(q0, nq), :] = (acc / l).astype(o_ref.dtype)
        return carry

    lax.fori_loop(0, nblk // unroll, body, 0)


def _natten(qkv, qkv_ctx, rpb, nb, seq, ctx_len, *, qr):
    heads, dh = NA_HEADS, HEAD_DIM
    rows = seq // GRID_W
    wr, w0s, case_of, cases = _natten_plan(rows, qr)
    nblk = rows // qr
    tab = _natten_tables(rpb, cases, qr, wr)
    ncase = len(cases)
    grid_spec = pltpu.PrefetchScalarGridSpec(
        num_scalar_prefetch=2,
        grid=(heads, nb),
        in_specs=[
            pl.BlockSpec((seq, dh), lambda h, b, *_: (b, h)),
            pl.BlockSpec((seq, dh), lambda h, b, *_: (b, heads + h)),
            pl.BlockSpec((seq, dh), lambda h, b, *_: (b, 2 * heads + h)),
            pl.BlockSpec((ctx_len, dh), lambda h, b, *_: (b, heads + h)),
            pl.BlockSpec((ctx_len, dh), lambda h, b, *_: (b, 2 * heads + h)),
            pl.BlockSpec((None, ncase, qr * GRID_W, wr * GRID_W), lambda h, b, *_: (h, 0, 0, 0)),
        ],
        out_specs=pl.BlockSpec((seq, dh), lambda h, b, *_: (b, h)),
    )
    return pl.pallas_call(
        functools.partial(_natten_kernel, qr=qr, wr=wr, nblk=nblk, scale=dh ** -0.5,
                          kch=_tile(wr * GRID_W, ATTN_KEY_CHUNK)),
        grid_spec=grid_spec,
        out_shape=jax.ShapeDtypeStruct((nb * seq, heads * dh), BF16),
        compiler_params=_cparams("parallel", "arbitrary"),
        name="natten",
    )(jnp.asarray(w0s), jnp.asarray(case_of), qkv, qkv, qkv, qkv_ctx, qkv_ctx, tab)


def _rope_tables(seq):
    t = jnp.arange(seq)
    quarter = HEAD_DIM // 4
    inv = ROPE_THETA ** (-jnp.arange(quarter, dtype=F32) / quarter)
    cos, sin = [], []
    for pos in (t // GRID_W, t % GRID_W):
        ang = pos.astype(F32)[:, None] * inv[None, :]
        cos += [jnp.cos(ang), jnp.cos(ang)]
        sin += [-jnp.sin(ang), jnp.sin(ang)]
    return jnp.concatenate(cos, axis=-1), jnp.concatenate(sin, axis=-1)


def _gqa_proj_kernel(*refs, rope, n_q, n_kv):
    if rope:
        x_ref, mod_ref, g_ref, w_ref, gq_ref, gk_ref, cos_ref, sin_ref, o_ref = refs
    else:
        x_ref, mod_ref, g_ref, w_ref, gq_ref, gk_ref, o_ref = refs
    dh = HEAD_DIM
    xn = _adanorm(x_ref[...], g_ref[...], mod_ref, 1).astype(BF16)
    if rope:
        cos = cos_ref[...]
        sin = sin_ref[...]
        lane = lax.broadcasted_iota(jnp.int32, cos.shape, 1)
        low_quarter = (lane % (dh // 2)) < (dh // 4)
    grp = 4
    for c in range((n_q + 2 * n_kv) // grp):
        y = _dot(xn, w_ref[:, c * grp * dh:(c + 1) * grp * dh])
        for hh in range(grp):
            head = c * grp + hh
            yh = y[:, hh * dh:(hh + 1) * dh]
            if head < n_q + n_kv:
                yh = _rms(yh, gq_ref[...] if head < n_q else gk_ref[...])
                if rope:
                    partner = jnp.where(low_quarter, pltpu.roll(yh, dh - dh // 4, 1), pltpu.roll(yh, dh // 4, 1))
                    yh = yh * cos + partner * sin
            o_ref[:, head * dh:(head + 1) * dh] = yh.astype(o_ref.dtype)


def _gqa_proj(x, mod, g, w, gq, gk, rope_tabs, *, tps, tm):
    n, d = x.shape
    n_out = w.shape[1]
    rope = rope_tabs is not None
    in_specs = [
        pl.BlockSpec((tm, d), lambda i: (i, 0)),
        pl.BlockSpec((None, 9, d), lambda i: (i // tps, 0, 0)),
        pl.BlockSpec((1, d), lambda i: (0, 0)),
        pl.BlockSpec((d, n_out), lambda i: (0, 0)),
        pl.BlockSpec((1, HEAD_DIM), lambda i: (0, 0)),
        pl.BlockSpec((1, HEAD_DIM), lambda i: (0, 0)),
    ]
    args = [x, mod, g.reshape(1, d), w, gq.reshape(1, -1), gk.reshape(1, -1)]
    if rope:
        in_specs += [pl.BlockSpec((tm, HEAD_DIM), lambda i: (i % tps, 0))] * 2
        args += list(rope_tabs)
    return pl.pallas_call(
        functools.partial(_gqa_proj_kernel, rope=rope, n_q=GA_HEADS, n_kv=GA_KV_HEADS),
        grid=(n // tm,),
        in_specs=in_specs,
        out_specs=pl.BlockSpec((tm, n_out), lambda i: (i, 0)),
        out_shape=jax.ShapeDtypeStruct((n, n_out), BF16),
        compiler_params=_cparams("parallel"),
        name="gqa_proj",
    )(*args)


def _conv_out_kernel(zp_ref, zc_ref, zn_ref, wdw_ref, bdw_ref, lg_ref, lb_ref, w2_ref, b2_ref, x_ref, mod_ref,
                     o_ref, ext_ref, sh_ref, cv_ref, *, tps, tm):
    d = zc_ref.shape[1]
    c = pl.program_id(0) % tps
    half = CV_KERNEL // 2
    ext_ref[0:CONV_HALO, :] = jnp.where(c > 0, zp_ref[...], 0.0)
    ext_ref[CONV_HALO:CONV_HALO + tm, :] = zc_ref[...]
    ext_ref[CONV_HALO + tm:, :] = jnp.where(c < tps - 1, zn_ref[...], 0.0)
    n_sh = sh_ref.shape[1]
    for b in range(1, SUBLANES):
        sh_ref[b - 1, :, :] = ext_ref[b:b + n_sh, :]
    rb, cb = 64, 512
    for r0 in range(0, tm, rb):
        for c0 in range(0, d, cb):
            acc = jnp.zeros((rb // SUBLANES, SUBLANES, cb), F32)
            for k in range(CV_KERNEL):
                off = CONV_HALO - half + r0 + k
                a, b = off // SUBLANES * SUBLANES, off % SUBLANES
                src = ext_ref if b == 0 else sh_ref.at[b - 1]
                tap = src[a:a + rb, c0:c0 + cb].reshape(rb // SUBLANES, SUBLANES, cb)
                acc = acc + wdw_ref[k, :, c0:c0 + cb][None] * tap
            cv_ref[r0:r0 + rb, c0:c0 + cb] = acc.reshape(rb, cb) + bdw_ref[:, c0:c0 + cb]
    z = cv_ref[...]
    mu = jnp.mean(z, axis=-1, keepdims=True)
    zc = z - mu
    var = jnp.mean(zc * zc, axis=-1, keepdims=True)
    zn = zc * lax.rsqrt(var + EPS) * lg_ref[...] + lb_ref[...]
    a = (zn * _sigmoid(zn)).astype(BF16)
    y = _dot(a, w2_ref[...]) + b2_ref[...]
    o_ref[...] = x_ref[...] + _gate_row(mod_ref, 1) * y


def _conv_out(z, wdw, bdw, ln_g, ln_b, w2, b2, x, mod, *, tps, mod_tps, tm):
    n, d = x.shape
    hb = tm // CONV_HALO
    last_hb = n // CONV_HALO - 1
    row = lambda v: v.reshape(1, d)
    return pl.pallas_call(
        functools.partial(_conv_out_kernel, tps=tps, tm=tm),
        grid=(n // tm,),
        in_specs=[
            pl.BlockSpec((CONV_HALO, d), lambda i: (jnp.maximum(i * hb - 1, 0), 0)),
            pl.BlockSpec((tm, d), lambda i: (i, 0)),
            pl.BlockSpec((CONV_HALO, d), lambda i: (jnp.minimum((i + 1) * hb, last_hb), 0)),
            pl.BlockSpec((CV_KERNEL, SUBLANES, d), lambda i: (0, 0, 0)),
            pl.BlockSpec((1, d), lambda i: (0, 0)),
            pl.BlockSpec((1, d), lambda i: (0, 0)),
            pl.BlockSpec((1, d), lambda i: (0, 0)),
            pl.BlockSpec((d, d), lambda i: (0, 0)),
            pl.BlockSpec((1, d), lambda i: (0, 0)),
            pl.BlockSpec((tm, d), lambda i: (i, 0)),
            pl.BlockSpec((None, 9, d), lambda i: (i // mod_tps, 0, 0)),
        ],
        out_specs=pl.BlockSpec((tm, d), lambda i: (i, 0)),
        out_shape=jax.ShapeDtypeStruct((n, d), F32),
        scratch_shapes=[
            pltpu.VMEM((tm + 2 * CONV_HALO, d), F32),
            pltpu.VMEM((SUBLANES - 1, tm + 2 * CONV_HALO - SUBLANES, d), F32),
            pltpu.VMEM((tm, d), F32),
        ],
        compiler_params=_cparams("parallel"),
        name="conv_module_tail",
    )(z, z, z, jnp.broadcast_to(wdw[:, None, :], (CV_KERNEL, SUBLANES, d)), row(bdw), row(ln_g), row(ln_b), w2,
      row(b2), x, mod)


def _softplus(x):
    return jnp.maximum(x, 0.0) + jnp.log1p(jnp.exp(-jnp.abs(x)))


def _lru_kernel(*refs, rev, tt, nt, add_prev):
    refs = list(refs)
    halo_ref, u_ref, cw_ref, cb_ref, wa_ref, ba_ref, wi_ref, bi_ref, lam_ref, h0_ref = refs[:10]
    refs = refs[10:]
    yp_ref = refs.pop(0) if add_prev else None
    y_ref, hl_ref, ext_ref, a_ref, b_ref, h_ref = refs
    i = pl.program_id(1)
    first = i == 0
    blk = wa_ref.shape[1]

    @pl.when(first)
    def _():
        h_ref[...] = h0_ref[...]

    halo = jnp.where(first, 0.0, halo_ref[...])
    if rev:
        ext_ref[0:tt, :] = u_ref[...]
        ext_ref[tt:tt + LRU_HALO, :] = halo
        offs = [LRU_CONV - 1 - k for k in range(LRU_CONV)]
    else:
        ext_ref[0:LRU_HALO, :] = halo
        ext_ref[LRU_HALO:LRU_HALO + tt, :] = u_ref[...]
        offs = [LRU_HALO - (LRU_CONV - 1) + k for k in range(LRU_CONV)]
    sp = _softplus(-lam_ref[...])
    for n in range(LRU_BLOCKS):
        cs = slice(n * blk, (n + 1) * blk)
        ud = cb_ref[:, cs] + cw_ref[0:1, cs] * ext_ref[offs[0]:offs[0] + tt, cs]
        for k in range(1, LRU_CONV):
            ud = ud + cw_ref[k:k + 1, cs] * ext_ref[offs[k]:offs[k] + tt, cs]
        zb = ud.astype(BF16)
        r = _sigmoid(_dot(zb, wa_ref[n]) + ba_ref[:, cs])
        ig = _sigmoid(_dot(zb, wi_ref[n]) + bi_ref[:, cs])
        log_a = (-LRU_C * r) * sp[:, cs]
        th = jnp.tanh(log_a)
        a_ref[:, cs] = jnp.exp(log_a)
        b_ref[:, cs] = jnp.sqrt((-2.0 * th) / (1.0 - th)) * (ig * ud)

    ngrp = tt // SUBLANES

    def group(gi, h):
        g = (ngrp - 1 - gi) if rev else gi
        base = pl.multiple_of(g * SUBLANES, SUBLANES)
        a8 = a_ref.at[pl.ds(base, SUBLANES)]
        b8 = b_ref.at[pl.ds(base, SUBLANES)]
        y8 = y_ref.at[pl.ds(base, SUBLANES)]
        order = range(SUBLANES - 1, -1, -1) if rev else range(SUBLANES)
        for j in order:
            h = a8[j:j + 1, :] * h + b8[j:j + 1, :]
            if add_prev:
                y8[j:j + 1, :] = h + yp_ref.at[pl.ds(base, SUBLANES)][j:j + 1, :]
            else:
                y8[j:j + 1, :] = h
        return h

    h = lax.fori_loop(0, ngrp, group, h_ref[...])
    h_ref[...] = h

    @pl.when(i == nt - 1)
    def _():
        hl_ref[...] = h


def _lru_scan(u, conv_w, conv_b, wa, ba, wi, bi, lam, h0, y_prev, nb, seq, *, rev, tt):
    r = wa.shape[0] * wa.shape[1]
    nt = seq // tt
    hb = tt // LRU_HALO
    last_hb = (nb * seq) // LRU_HALO - 1
    tile = (lambda b, i: b * nt + (nt - 1 - i)) if rev else (lambda b, i: b * nt + i)
    if rev:
        halo_map = lambda b, i: (jnp.minimum((tile(b, i) + 1) * hb, last_hb), 0)
    else:
        halo_map = lambda b, i: (jnp.maximum(tile(b, i) * hb - 1, 0), 0)
    row = lambda v: v.reshape(1, r)
    vec = lambda: pl.BlockSpec((1, r), lambda b, i: (0, 0))
    in_specs = [
        pl.BlockSpec((LRU_HALO, r), halo_map),
        pl.BlockSpec((tt, r), lambda b, i: (tile(b, i), 0)),
        pl.BlockSpec((LRU_CONV, r), lambda b, i: (0, 0)),
        vec(),
        pl.BlockSpec(wa.shape, lambda b, i: (0, 0, 0)),
        vec(),
        pl.BlockSpec(wi.shape, lambda b, i: (0, 0, 0)),
        vec(),
        vec(),
        pl.BlockSpec((None, 1, r), lambda b, i: (b, 0, 0)),
    ]
    args = [u, u, conv_w, row(conv_b), wa, row(ba), wi, row(bi), row(lam), h0]
    add_prev = y_prev is not None
    if add_prev:
        in_specs.append(pl.BlockSpec((tt, r), lambda b, i: (tile(b, i), 0)))
        args.append(y_prev)
    return pl.pallas_call(
        functools.partial(_lru_kernel, rev=rev, tt=tt, nt=nt, add_prev=add_prev),
        grid=(nb, nt),
        in_specs=in_specs,
        out_specs=[
            pl.BlockSpec((tt, r), lambda b, i: (tile(b, i), 0)),
            pl.BlockSpec((None, 1, r), lambda b, i: (b, 0, 0)),
        ],
        out_shape=[jax.ShapeDtypeStruct((nb * seq, r), F32), jax.ShapeDtypeStruct((nb, 1, r), F32)],
        scratch_shapes=[
            pltpu.VMEM((tt + LRU_HALO, r), F32),
            pltpu.VMEM((tt, r), F32),
            pltpu.VMEM((tt, r), F32),
            pltpu.VMEM((1, r), F32),
        ],
        compiler_params=_cparams("parallel", "arbitrary"),
        name="rglru_scan",
    )(*args)


def _tile(n, pref):
    t = min(pref, n)
    while n % t:
        t //= 2
    return t


def kernel(x, c, ctx, c_ctx, ada_w, ada_b, norm_g, ffn_w13, ffn_w2, final_g, na_wqkv, na_wo, na_rpb, cv_w1, cv_b1, cv_wdw, cv_bdw, cv_ln_g, cv_ln_b, cv_w2, cv_b2, ga_wq, ga_wkv, ga_wo, ga_gq, ga_gk, lru_win, lru_convw, lru_convb, lru_wa, lru_ba, lru_wi, lru_bi, lru_lam, lru_wout):
    nb, seq, d = x.shape
    ctx_len = ctx.shape[1]
    depth = ada_w.shape[0]
    n_lat, n_ctx = nb * seq, nb * ctx_len
    xs = x.reshape(n_lat, d)
    hs = ctx.reshape(n_ctx, d)

    seg = -(-(nb + 1) // SUBLANES) * SUBLANES
    cvec = jnp.zeros((seg, d), F32).at[:nb].set(c).at[nb].set(c_ctx)
    mod = _ada_mod(cvec, ada_w, ada_b)

    w13_b, w2_b = ffn_w13.astype(BF16), ffn_w2.astype(BF16)
    ffn_hidden = ffn_w2.shape[2]

    tm_l = _tile(seq, 512)
    tm_c = _tile(n_ctx, 512)
    tf = 512 if ffn_hidden % 512 == 0 else 128
    tps_l = seq // tm_l
    tps_c = n_ctx // tm_c
    tm_p = _tile(seq, 1024)
    tps_p = seq // tm_p

    def ffn(v, mod_v, i, half, sub, tps, tm, fin=None):
        return _swiglu(v, mod_v, norm_g[i, sub], w13_b, w2_b, i, half, sub=sub, tps=tps, tm=tm, tf=tf, final_g=fin)

    for i in range(depth):
        kind, j = i % N_MIXERS, i // N_MIXERS
        last = i == depth - 1
        ctx_in = not (last and kind == 1)
        ctx_out = not last
        mod_l, mod_c = mod[i, :nb], mod[i, nb:nb + 1]
        g1 = norm_g[i, 1]

        xs = ffn(xs, mod_l, i, 0, 0, tps_l, tm_l)
        if ctx_in:
            hs = ffn(hs, mod_c, i, 0, 0, tps_c, tm_c)

        if kind == 0:
            wqkv = na_wqkv[j].astype(BF16)
            wo = na_wo[j].astype(BF16)
            pj = functools.partial(_proj, g=g1, w=wqkv, n_out=3 * d, out_dtype=BF16, tn=_tile(3 * d, 1536))
            qkv_l = pj(xs, mod_l, tps=tps_p, tm=tm_p)
            qkv_c = pj(hs, mod_c, tps=tps_c, tm=tm_c)
            rows = seq // GRID_W
            qr = 8 if rows % 8 == 0 and rows >= 16 else 1
            o_l = _natten(qkv_l, qkv_c, na_rpb[j], nb, seq, ctx_len, qr=qr)
            a_spec = lambda tm: [pl.BlockSpec((tm, d), lambda t: (t, 0))]
            xs = _outres([o_l], a_spec(tm_l), wo, None, xs, mod_l, mode="plain", tps=tps_l, tm=tm_l)
            if ctx_out:
                o_c = _attention(qkv_c, ctx_len, nb, heads=NA_HEADS, group=1, k_col=NA_HEADS, v_col=2 * NA_HEADS,
                                 tq=ctx_len)
                hs = _outres([o_c], a_spec(tm_c), wo, None, hs, mod_c, mode="plain", tps=tps_c, tm=tm_c)
        elif kind == 1:
            w1 = cv_w1[j].astype(BF16)
            w2 = cv_w2[j].astype(BF16)
            pj = functools.partial(_proj, g=g1, w=w1, bias=cv_b1[j], n_out=d, glu=True, out_dtype=F32, tn=512)
            tail = functools.partial(_conv_out, wdw=cv_wdw[j], bdw=cv_bdw[j], ln_g=cv_ln_g[j], ln_b=cv_ln_b[j],
                                     w2=w2, b2=cv_b2[j])
            tc_l, tc_c = _tile(seq, 256), _tile(ctx_len, 256)
            z_l = pj(xs, mod_l, tps=tps_p, tm=tm_p)
            xs = tail(z_l, x=xs, mod=mod_l, tps=seq // tc_l, mod_tps=seq // tc_l, tm=tc_l)
            if ctx_out:
                z_c = pj(hs, mod_c, tps=tps_c, tm=tm_c)
                hs = tail(z_c, x=hs, mod=mod_c, tps=ctx_len // tc_c, mod_tps=n_ctx // tc_c, tm=tc_c)
        elif kind == 2:
            wqkv = jnp.concatenate([ga_wq[j], ga_wkv[j]], axis=1).astype(BF16)
            wo = ga_wo[j].astype(BF16)
            nqc = GA_HEADS
            p_l = _gqa_proj(xs, mod_l, g1, wqkv, ga_gq[j], ga_gk[j], _rope_tables(seq), tps=tps_l, tm=tm_l)
            p_c = _gqa_proj(hs, mod_c, g1, wqkv, ga_gq[j], ga_gk[j], None, tps=tps_c, tm=tm_c)
            grp = GA_HEADS // GA_KV_HEADS
            o_l = _attention(p_l, seq, nb, heads=GA_HEADS, group=grp, k_col=nqc, v_col=nqc + GA_KV_HEADS,
                             tq=_tile(seq, 1024), ctx=p_c, ctx_len=ctx_len)
            a_spec = lambda tm: [pl.BlockSpec((tm, d), lambda t: (t, 0))]
            xs = _outres([o_l], a_spec(tm_l), wo, None, xs, mod_l, mode="plain", tps=tps_l, tm=tm_l)
            if ctx_out:
                o_c = _attention(p_c, ctx_len, nb, heads=GA_HEADS, group=grp, k_col=nqc, v_col=nqc + GA_KV_HEADS,
                                 tq=ctx_len)
                hs = _outres([o_c], a_spec(tm_c), wo, None, hs, mod_c, mode="plain", tps=tps_c, tm=tm_c)
        else:
            if ctx_out:
                raise NotImplementedError("RG-LRU context output is never needed: the LRU layer is the last one")
            r = lru_win.shape[2] // 2
            w_in = lru_win[j].astype(BF16)
            w_out = lru_wout[j].astype(BF16)
            wa, wi = lru_wa[j].astype(BF16), lru_wi[j].astype(BF16)
            tn = _tile(r, 1280)
            u_c = _proj(hs, mod_c, g1, w_in, n_out=r, out_dtype=F32, tps=tps_c, tm=tm_c, tn=tn)
            u_l = _proj(xs, mod_l, g1, w_in, n_out=2 * r, out_dtype=F32, tps=tps_p, tm=tm_p, tn=tn)
            tt_l, tt_c = _tile(seq, 256), _tile(ctx_len, 256)
            zeros = jnp.zeros((nb, 1, r), F32)
            y = None
            for e in range(2):
                prm = (lru_convw[j, e], lru_convb[j, e], wa[e], lru_ba[j, e], wi[e], lru_bi[j, e], lru_lam[j, e])
                _, h_ctx = _lru_scan(u_c, *prm, zeros, None, nb, ctx_len, rev=bool(e), tt=tt_c)
                y, _ = _lru_scan(u_l, *prm, h_ctx, y, nb, seq, rev=bool(e), tt=tt_l)
            tm_o = _tile(seq, 256)
            a_specs = [pl.BlockSpec((tm_o, r), lambda t: (t, 1)), pl.BlockSpec((tm_o, r), lambda t: (t, 0))]
            xs = _outres([u_l, y], a_specs, w_out, None, xs, mod_l, mode="gelu_mul", tps=seq // tm_o, tm=tm_o)

        xs = ffn(xs, mod_l, i, 1, 2, tps_l, tm_l, fin=final_g if last else None)
        if ctx_out:
            hs = ffn(hs, mod_c, i, 1, 2, tps_c, tm_c)
    return xs.reshape(nb, seq, d)
```
